```python
import jax, jax.numpy as jnp
from jax import lax
import numpy as np

D_MODEL = 1024
BATCH = 4
SEQ = 4096
DEPTH = 1

HEAD_DIM = 64
SB_HEADS = 8
SWA_HEADS = 8
SWA_KV_HEADS = 2
SWA_GROUP = SWA_HEADS // SWA_KV_HEADS
D_SB = SB_HEADS * HEAD_DIM
D_SWA = SWA_HEADS * HEAD_DIM
D_SWA_KV = SWA_KV_HEADS * HEAD_DIM
D_MIX = D_SB + D_SWA
D_IN = 3 * D_SB + D_SWA + 2 * D_SWA_KV
Q_BLOCK = 128
WINDOW = 128
PEER_HEADS = 8
N_KEYS = 128
N_EXPERTS = N_KEYS * N_KEYS
PEER_TOPK = 16
D_KEY = 256
D_HALF = D_KEY // 2
TOKEN_CHUNK = 128
EPS = 1e-6

kernel_name = "hybrid_stickbreak_swa_peer_block"


def rmsnorm(x, g):
    xf = x.astype(jnp.float32)
    y = xf * lax.rsqrt(jnp.mean(xf * xf, axis=-1, keepdims=True) + EPS)
    return (y * g.astype(jnp.float32)).astype(x.dtype)


def head_rmsnorm(o, g, n_heads):
    B, S, _ = o.shape
    of = o.reshape(B, S, n_heads, HEAD_DIM).astype(jnp.float32)
    y = of * lax.rsqrt(jnp.mean(of * of, axis=-1, keepdims=True) + EPS)
    y = y * g.reshape(n_heads, HEAD_DIM).astype(jnp.float32)
    return y.reshape(B, S, n_heads * HEAD_DIM).astype(o.dtype)


def alibi_slopes(n_heads):
    return jnp.exp2(-8.0 * jnp.arange(1, n_heads + 1, dtype=jnp.float32) / n_heads)


def stick_breaking_attention(q, k, v):
    B, S, H, Dh = q.shape
    nblk = S // Q_BLOCK
    scale = Dh ** -0.5
    qb = q.reshape(B, nblk, Q_BLOCK, H, Dh).transpose(1, 0, 3, 2, 4)
    kh = k.transpose(0, 2, 1, 3)
    vh = v.transpose(0, 2, 1, 3)
    key_pos = jnp.arange(S)

    def block(args):
        qi, i = args
        z = jnp.einsum('bhqd,bhkd->bhqk', qi, kh, preferred_element_type=jnp.float32) * scale
        q_pos = i * Q_BLOCK + jnp.arange(Q_BLOCK)
        mask = key_pos[None, :] < q_pos[:, None]
        log_1m = jnp.where(mask, jax.nn.log_sigmoid(-z), 0.0)
        later = lax.cumsum(log_1m, axis=3, reverse=True) - log_1m
        a = jnp.where(mask, jnp.exp(jax.nn.log_sigmoid(z) + later), 0.0)
        return jnp.einsum('bhqk,bhkd->bhqd', a.astype(vh.dtype), vh)

    out = lax.map(block, (qb, jnp.arange(nblk)))
    return out.transpose(1, 0, 3, 2, 4).reshape(B, S, H * Dh)


def sliding_window_attention(q, k, v, sinks):
    B, S, H, Dh = q.shape
    KV = k.shape[2]
    G = H // KV
    n = S // Q_BLOCK
    qb = q.reshape(B, n, Q_BLOCK, KV, G, Dh)

    def with_prev(t):
        tb = t.reshape(B, n, Q_BLOCK, KV, Dh)
        prev = jnp.pad(tb[:, :-1], ((0, 0), (1, 0), (0, 0), (0, 0), (0, 0)))
        return jnp.concatenate([prev, tb], axis=2)

    kb, vb = with_prev(k), with_prev(v)
    s = jnp.einsum('bnqkgd,bnskd->bnkgqs', qb, kb, preferred_element_type=jnp.float32) * Dh ** -0.5
    dist = jnp.arange(Q_BLOCK)[:, None] + Q_BLOCK - jnp.arange(2 * Q_BLOCK)[None, :]
    key_abs = jnp.arange(n)[:, None] * Q_BLOCK + jnp.arange(2 * Q_BLOCK)[None, :] - Q_BLOCK
    mask = ((dist >= 0) & (dist < WINDOW))[None, :, :] & (key_abs >= 0)[:, None, :]
    slopes = alibi_slopes(H).reshape(KV, G)
    s = s - slopes[:, :, None, None] * dist.astype(jnp.float32)
    s = jnp.where(mask[None, :, None, None, :, :], s, -jnp.inf)
    sink = jnp.broadcast_to(sinks.astype(jnp.float32).reshape(KV, G)[None, None, :, :, None, None],
                            s.shape[:-1] + (1,))
    p = jax.nn.softmax(jnp.concatenate([s, sink], axis=-1), axis=-1)[..., :-1]
    out = jnp.einsum('bnkgqs,bnskd->bnqkgd', p.astype(vb.dtype), vb)
    return out.reshape(B, S, H * Dh)


def peer_ffn(x, w_q, keys1, keys2, u, v):
    B, S, D = x.shape
    T = B * S
    xt = x.reshape(T, D)
    q = (xt @ w_q).reshape(T, PEER_HEADS, 2, D_HALF)
    s1 = jnp.einsum('thd,hnd->thn', q[:, :, 0], keys1, preferred_element_type=jnp.float32)
    s2 = jnp.einsum('thd,hnd->thn', q[:, :, 1], keys2, preferred_element_type=jnp.float32)
    v1, i1 = lax.top_k(s1, PEER_TOPK)
    v2, i2 = lax.top_k(s2, PEER_TOPK)
    cand = (v1[..., :, None] + v2[..., None, :]).reshape(T, PEER_HEADS, PEER_TOPK * PEER_TOPK)
    cand_idx = (i1[..., :, None] * N_KEYS + i2[..., None, :]).reshape(T, PEER_HEADS, PEER_TOPK * PEER_TOPK)
    top_s, pos = lax.top_k(cand, PEER_TOPK)
    idx = jnp.take_along_axis(cand_idx, pos, axis=-1)
    gate = jax.nn.softmax(top_s, axis=-1)
    nc = T // TOKEN_CHUNK

    def chunk(args):
        xc, ic, gc = args
        h = jax.nn.gelu(jnp.einsum('cd,chkd->chk', xc, u[ic]))
        return jnp.einsum('chk,chkd->cd', (gc * h).astype(v.dtype), v[ic])

    out = lax.map(chunk, (xt.reshape(nc, TOKEN_CHUNK, D),
                          idx.reshape(nc, TOKEN_CHUNK, PEER_HEADS, PEER_TOPK),
                          gate.reshape(nc, TOKEN_CHUNK, PEER_HEADS, PEER_TOPK)))
    return out.reshape(B, S, D).astype(x.dtype)


def setup_inputs(seed: int = 0) -> dict:
    key = jax.random.key(seed)
    ks = jax.random.split(key, 14)
    f32 = jnp.float32
    L = DEPTH
    return {
        "x": jax.random.normal(ks[0], (BATCH, SEQ, D_MODEL), f32),
        "norm1_g": 1.0 + 0.02 * jax.random.normal(ks[1], (L, D_MODEL), f32),
        "w_in": jax.random.normal(ks[2], (L, D_MODEL, D_IN), f32) * D_MODEL ** -0.5,
        "sb_out_g": 1.0 + 0.02 * jax.random.normal(ks[3], (L, D_SB), f32),
        "swa_out_g": 1.0 + 0.02 * jax.random.normal(ks[4], (L, D_SWA), f32),
        "swa_sinks": 0.5 * jax.random.normal(ks[5], (L, SWA_HEADS), f32),
        "w_out": jax.random.normal(ks[6], (L, D_MIX, D_MODEL), f32) * D_MIX ** -0.5,
        "norm2_g": 1.0 + 0.02 * jax.random.normal(ks[7], (L, D_MODEL), f32),
        "peer_w_q": jax.random.normal(ks[8], (L, D_MODEL, PEER_HEADS * D_KEY), f32) * D_MODEL ** -0.5,
        "peer_keys1": jax.random.normal(ks[9], (L, PEER_HEADS, N_KEYS, D_HALF), f32) * D_HALF ** -0.5,
        "peer_keys2": jax.random.normal(ks[10], (L, PEER_HEADS, N_KEYS, D_HALF), f32) * D_HALF ** -0.5,
        "peer_u": jax.random.normal(ks[11], (L, N_EXPERTS, D_MODEL), f32) * D_MODEL ** -0.5,
        "peer_v": jax.random.normal(ks[12], (L, N_EXPERTS, D_MODEL), f32) * PEER_HEADS ** -0.5,
        "final_g": 1.0 + 0.02 * jax.random.normal(ks[13], (D_MODEL,), f32),
    }


def reference(x, norm1_g, w_in, sb_out_g, swa_out_g, swa_sinks, w_out, norm2_g,
              peer_w_q, peer_keys1, peer_keys2, peer_u, peer_v, final_g):
    B, S, _ = x.shape
    splits = [D_SB, 2 * D_SB, 3 * D_SB, 3 * D_SB + D_SWA, 3 * D_SB + D_SWA + D_SWA_KV]
    for l in range(DEPTH):
        h = rmsnorm(x, norm1_g[l])
        proj = h @ w_in[l]
        sb_q, sb_k, sb_v, sw_q, sw_k, sw_v = jnp.split(proj, splits, axis=-1)
        sb = stick_breaking_attention(sb_q.reshape(B, S, SB_HEADS, HEAD_DIM),
                                      sb_k.reshape(B, S, SB_HEADS, HEAD_DIM),
                                      sb_v.reshape(B, S, SB_HEADS, HEAD_DIM))
        sw = sliding_window_attention(sw_q.reshape(B, S, SWA_HEADS, HEAD_DIM),
                                      sw_k.reshape(B, S, SWA_KV_HEADS, HEAD_DIM),
                                      sw_v.reshape(B, S, SWA_KV_HEADS, HEAD_DIM),
                                      swa_sinks[l])
        mixed = jnp.concatenate([head_rmsnorm(sb, sb_out_g[l], SB_HEADS),
                                 head_rmsnorm(sw, swa_out_g[l], SWA_HEADS)], axis=-1)
        x = x + mixed @ w_out[l]
        x = x + peer_ffn(rmsnorm(x, norm2_g[l]), peer_w_q[l], peer_keys1[l], peer_keys2[l],
                         peer_u[l], peer_v[l])
    return rmsnorm(x, final_g)
```

```python
import functools
import math

import jax
import jax.numpy as jnp
from jax import lax
from jax.experimental import pallas as pl
from jax.experimental.pallas import tpu as pltpu

F32 = jnp.float32
BF16 = jnp.bfloat16

HEAD_DIM = 64
SB_HEADS = 8
SWA_HEADS = 8
SWA_KV_HEADS = 2
D_SB = SB_HEADS * HEAD_DIM
D_SWA = SWA_HEADS * HEAD_DIM
D_SWA_KV = SWA_KV_HEADS * HEAD_DIM
Q_BLOCK = 128
WINDOW = 128
PEER_HEADS = 8
N_KEYS = 128
PEER_TOPK = 16
D_HALF = 128
EPS = 1e-6

LANES = 128
SB_KEY_WINDOW = 256
SB_DEAD_LOG = -120.0
NEG_BIG = -1e30

VMEM_LIMIT = 48 * 1024 * 1024


def _dot(a, b):
    return jnp.dot(a, b, preferred_element_type=F32)


def _dot_nt(a, b):
    return lax.dot_general(a, b, (((1,), (1,)), ((), ())), preferred_element_type=F32)


def _dot_tn(a, b):
    return lax.dot_general(a, b, (((0,), (0,)), ((), ())), preferred_element_type=F32)


def _rms_scale(y):
    return lax.rsqrt(jnp.mean(y * y, axis=-1, keepdims=True) + EPS)


def _pair_head_rmsnorm(o, gain, lo):
    sq = o * o
    ss_lo = jnp.sum(jnp.where(lo, sq, 0.0), axis=1, keepdims=True)
    ss_hi = jnp.sum(jnp.where(lo, 0.0, sq), axis=1, keepdims=True)
    ms = jnp.where(lo, ss_lo, ss_hi) * (1.0 / HEAD_DIM)
    return o * lax.rsqrt(ms + EPS) * gain


def _in_proj_kernel(x_ref, g_ref, w_ref, o_ref):
    x = x_ref[...]
    h = (x * _rms_scale(x) * g_ref[...]).astype(BF16)
    o_ref[...] = _dot(h, w_ref[...]).astype(o_ref.dtype)


def _in_proj(xt, gain, w, tm):
    t, d = xt.shape
    n = w.shape[1]
    return pl.pallas_call(
        _in_proj_kernel,
        grid=(t // tm,),
        in_specs=[pl.BlockSpec((tm, d), lambda i: (i, 0)),
                  pl.BlockSpec((1, d), lambda i: (0, 0)),
                  pl.BlockSpec((d, n), lambda i: (0, 0))],
        out_specs=pl.BlockSpec((tm, n), lambda i: (i, 0)),
        out_shape=jax.ShapeDtypeStruct((t, n), BF16),
        compiler_params=pltpu.CompilerParams(dimension_semantics=("parallel",),
                                             vmem_limit_bytes=VMEM_LIMIT),
        name="in_proj",
    )(xt, gain, w)


def _sb_kernel(q_ref, k_ref, v_ref, g_ref, o_ref, acc_ref, carry_ref):
    kw = SB_KEY_WINDOW
    i = pl.program_id(2)
    lane = lax.broadcasted_iota(jnp.int32, (Q_BLOCK, LANES), 1)
    lo = lane < HEAD_DIM
    q = q_ref[0] * jnp.asarray(HEAD_DIM ** -0.5, BF16)
    zero = jnp.zeros_like(q)
    q2 = jnp.concatenate([jnp.where(lo, q, zero), jnp.where(lo, zero, q)], axis=0)

    row = lax.broadcasted_iota(jnp.int32, (2 * Q_BLOCK, kw), 0)
    col = lax.broadcasted_iota(jnp.int32, (2 * Q_BLOCK, kw), 1)
    qpos = i * Q_BLOCK + (row & (Q_BLOCK - 1))
    after = (lax.broadcasted_iota(jnp.int32, (kw, kw), 0)
             > lax.broadcasted_iota(jnp.int32, (kw, kw), 1)).astype(BF16)

    acc_ref[...] = jnp.zeros_like(acc_ref)
    carry_ref[...] = jnp.zeros_like(carry_ref)

    def body(state):
        w, _ = state
        start = pl.multiple_of(w * kw, kw)
        kb = k_ref[0, pl.ds(start, kw), :]
        vb = v_ref[0, pl.ds(start, kw), :]
        z = _dot_nt(q2, kb)
        mask = (start + col) < qpos
        t = jnp.log(1.0 + jnp.exp(-jnp.abs(z)))
        log_1m = jnp.where(mask, -jnp.maximum(z, 0.0) - t, 0.0)
        log_b = jnp.minimum(z, 0.0) - t
        hi = log_1m.astype(BF16)
        lo_part = (log_1m - hi.astype(F32)).astype(BF16)
        inner = _dot(hi, after) + _dot(lo_part, after)
        carry = carry_ref[...]
        a = jnp.where(mask, jnp.exp(log_b + inner + carry), 0.0)
        acc_ref[...] += _dot(a.astype(BF16), vb)
        carry = carry + jnp.sum(log_1m, axis=1, keepdims=True)
        carry_ref[...] = carry
        return w - 1, jnp.max(carry) > SB_DEAD_LOG

    w0 = (i * Q_BLOCK) // kw
    lax.while_loop(lambda s: jnp.logical_and(s[0] >= 0, s[1]), body, (w0, True))

    acc = acc_ref[...]
    o = jnp.where(lo, acc[:Q_BLOCK], acc[Q_BLOCK:])
    o_ref[0] = _pair_head_rmsnorm(o, g_ref[...], lo).astype(o_ref.dtype)


def _sb_attention(proj3, gain):
    b, s, _ = proj3.shape
    nq = s // Q_BLOCK
    pairs = D_SB // LANES
    return pl.pallas_call(
        _sb_kernel,
        grid=(b, pairs, nq),
        in_specs=[pl.BlockSpec((1, Q_BLOCK, LANES), lambda bi, p, i: (bi, i, p)),
                  pl.BlockSpec((1, s, LANES), lambda bi, p, i: (bi, 0, pairs + p)),
                  pl.BlockSpec((1, s, LANES), lambda bi, p, i: (bi, 0, 2 * pairs + p)),
                  pl.BlockSpec((1, LANES), lambda bi, p, i: (0, p))],
        out_specs=pl.BlockSpec((1, Q_BLOCK, LANES), lambda bi, p, i: (bi, i, p)),
        out_shape=jax.ShapeDtypeStruct((b, s, D_SB), BF16),
        scratch_shapes=[pltpu.VMEM((2 * Q_BLOCK, LANES), F32),
                        pltpu.VMEM((2 * Q_BLOCK, 1), F32)],
        compiler_params=pltpu.CompilerParams(
            dimension_semantics=("parallel", "parallel", "arbitrary"),
            vmem_limit_bytes=VMEM_LIMIT),
        name="sb_attn",
    )(proj3, proj3, proj3, gain)


def _swa_kernel(sink_ref, q_ref, kp_ref, kc_ref, vp_ref, vc_ref, g_ref, o_ref):
    i = pl.program_id(1)
    k = jnp.concatenate([kp_ref[0], kc_ref[0]], axis=0)
    v = jnp.concatenate([vp_ref[0], vc_ref[0]], axis=0)
    perm = ((lax.broadcasted_iota(jnp.int32, (LANES, LANES), 0) + HEAD_DIM) % LANES
            == lax.broadcasted_iota(jnp.int32, (LANES, LANES), 1)).astype(BF16)
    k_sw = _dot(k, perm).astype(BF16)
    v_sw = _dot(v, perm).astype(BF16)

    r = lax.broadcasted_iota(jnp.int32, (Q_BLOCK, 2 * Q_BLOCK), 0)
    c = lax.broadcasted_iota(jnp.int32, (Q_BLOCK, 2 * Q_BLOCK), 1)
    dist = r + Q_BLOCK - c
    first_col = jnp.where(i > 0, 0, Q_BLOCK)
    valid = (dist >= 0) & (dist < WINDOW) & (c >= first_col)
    distf = dist.astype(F32)

    lane = lax.broadcasted_iota(jnp.int32, (Q_BLOCK, LANES), 1)
    lo = lane < HEAD_DIM
    heads_per_block = LANES // HEAD_DIM
    group = SWA_HEADS // SWA_KV_HEADS
    for blk in range(D_SWA // LANES):
        qb = q_ref[0, :, blk * LANES:(blk + 1) * LANES] * jnp.asarray(HEAD_DIM ** -0.5, BF16)
        zero = jnp.zeros_like(qb)
        outs = []
        for sub in range(heads_per_block):
            hq = blk * heads_per_block + sub
            kv = hq // group
            aligned = (kv == sub)
            kk = k if aligned else k_sw
            vv = v if aligned else v_sw
            qh = jnp.where(lo, qb, zero) if sub == 0 else jnp.where(lo, zero, qb)
            slope = 2.0 ** (-8.0 * (hq + 1) / SWA_HEADS)
            s = _dot_nt(qh, kk) - slope * distf
            s = jnp.where(valid, s, NEG_BIG)
            sink = sink_ref[0, hq]
            m = jnp.maximum(jnp.max(s, axis=1, keepdims=True), sink)
            p = jnp.exp(s - m)
            denom = jnp.sum(p, axis=1, keepdims=True) + jnp.exp(sink - m)
            outs.append(_dot(p.astype(BF16), vv) / denom)
        o = jnp.where(lo, outs[0], outs[1])
        gain = g_ref[:, blk * LANES:(blk + 1) * LANES]
        o_ref[0, :, blk * LANES:(blk + 1) * LANES] = _pair_head_rmsnorm(o, gain, lo).astype(o_ref.dtype)


def _swa_attention(proj3, sinks, gain):
    b, s, _ = proj3.shape
    nq = s // Q_BLOCK
    q_blk = (3 * D_SB) // D_SWA
    k_blk = (3 * D_SB + D_SWA) // LANES
    v_blk = k_blk + 1
    prev = lambda bi, i: (bi, jnp.maximum(i - 1, 0), k_blk)
    prev_v = lambda bi, i: (bi, jnp.maximum(i - 1, 0), v_blk)
    return pl.pallas_call(
        _swa_kernel,
        grid=(b, nq),
        in_specs=[pl.BlockSpec(memory_space=pltpu.SMEM),
                  pl.BlockSpec((1, Q_BLOCK, D_SWA), lambda bi, i: (bi, i, q_blk)),
                  pl.BlockSpec((1, Q_BLOCK, LANES), prev),
                  pl.BlockSpec((1, Q_BLOCK, LANES), lambda bi, i: (bi, i, k_blk)),
                  pl.BlockSpec((1, Q_BLOCK, LANES), prev_v),
                  pl.BlockSpec((1, Q_BLOCK, LANES), lambda bi, i: (bi, i, v_blk)),
                  pl.BlockSpec((1, D_SWA), lambda bi, i: (0, 0))],
        out_specs=pl.BlockSpec((1, Q_BLOCK, D_SWA), lambda bi, i: (bi, i, 0)),
        out_shape=jax.ShapeDtypeStruct((b, s, D_SWA), BF16),
        compiler_params=pltpu.CompilerParams(dimension_semantics=("parallel", "parallel"),
                                             vmem_limit_bytes=VMEM_LIMIT),
        name="swa_attn",
    )(sinks, proj3, proj3, proj3, proj3, proj3, gain)


def _out_proj_kernel(x_ref, sb_ref, sw_ref, wa_ref, wb_ref, g_ref, x1_ref, h2_ref):
    y = x_ref[...] + _dot(sb_ref[...], wa_ref[...]) + _dot(sw_ref[...], wb_ref[...])
    x1_ref[...] = y
    h2_ref[...] = (y * _rms_scale(y) * g_ref[...]).astype(h2_ref.dtype)


def _out_proj(xt, sbm, swm, wa, wb, gain, tm):
    t, d = xt.shape
    row = lambda i: (i, 0)
    fixed = lambda i: (0, 0)
    return pl.pallas_call(
        _out_proj_kernel,
        grid=(t // tm,),
        in_specs=[pl.BlockSpec((tm, d), row),
                  pl.BlockSpec((tm, sbm.shape[1]), row),
                  pl.BlockSpec((tm, swm.shape[1]), row),
                  pl.BlockSpec(wa.shape, fixed),
                  pl.BlockSpec(wb.shape, fixed),
                  pl.BlockSpec((1, d), fixed)],
        out_specs=[pl.BlockSpec((tm, d), row), pl.BlockSpec((tm, d), row)],
        out_shape=[jax.ShapeDtypeStruct((t, d), F32), jax.ShapeDtypeStruct((t, d), BF16)],
        compiler_params=pltpu.CompilerParams(dimension_semantics=("parallel",),
                                             vmem_limit_bytes=VMEM_LIMIT),
        name="out_proj",
    )(xt, sbm, swm, wa, wb, gain)


def _top_rows(s, count, want_rank):
    vals = []
    rank = jnp.full(s.shape, float(N_KEYS - 1), F32) if want_rank else None
    for k in range(count):
        mx = jnp.max(s, axis=0, keepdims=True)
        vals.append(mx)
        hit = s == mx
        if want_rank:
            rank = jnp.where(hit, float(k), rank)
        s = jnp.where(hit, NEG_BIG, s)
    return vals, rank


def _route_kernel(h_ref, wq_ref, k1_ref, k2_ref, rank_ref, d_ref, n_ref, c_ref):
    q = _dot(h_ref[...], wq_ref[...]).astype(BF16)
    s1 = _dot_nt(k1_ref[0], q[:, :D_HALF])
    s2 = _dot_nt(k2_ref[0], q[:, D_HALF:])
    v1, _ = _top_rows(s1, PEER_TOPK, False)
    v2, rank2 = _top_rows(s2, PEER_TOPK, True)
    v2_all = jnp.concatenate(v2, axis=0)
    cand = jnp.concatenate([v1[a] + v2_all for a in range(PEER_TOPK)], axis=0)
    top, _ = _top_rows(cand, PEER_TOPK, False)
    thr = top[PEER_TOPK - 1]
    z = jnp.exp(top[0] - top[0])
    for kk in range(1, PEER_TOPK):
        z = z + jnp.exp(top[kk] - top[0])
    n = jnp.zeros(s1.shape, F32)
    for j in range(PEER_TOPK):
        n = n + jnp.where(s1 + v2[j] >= thr, 1.0, 0.0)
    rank_ref[0] = rank2
    n_ref[0] = n
    c_ref[0] = jnp.exp(s1 - v1[0]) / z
    d_ref[0] = jnp.exp(s2 - v2[0])


def _peer_route(h2, wq, k1, k2, tm):
    t, d = h2.shape
    heads = k1.shape[0]
    tbl = jax.ShapeDtypeStruct((heads, N_KEYS, t), F32)
    tbl_spec = pl.BlockSpec((1, N_KEYS, tm), lambda i, h: (h, 0, i))
    return pl.pallas_call(
        _route_kernel,
        grid=(t // tm, heads),
        in_specs=[pl.BlockSpec((tm, d), lambda i, h: (i, 0)),
                  pl.BlockSpec((d, 2 * D_HALF), lambda i, h: (0, h)),
                  pl.BlockSpec((1, N_KEYS, D_HALF), lambda i, h: (h, 0, 0)),
                  pl.BlockSpec((1, N_KEYS, D_HALF), lambda i, h: (h, 0, 0))],
        out_specs=[tbl_spec, tbl_spec, tbl_spec, tbl_spec],
        out_shape=[tbl, tbl, tbl, tbl],
        compiler_params=pltpu.CompilerParams(dimension_semantics=("parallel", "arbitrary"),
                                             vmem_limit_bytes=VMEM_LIMIT),
        name="peer_route",
    )(h2, wq, k1, k2)


def _gelu_tanh(x):
    inner = math.sqrt(2.0 / math.pi) * (x + 0.044715 * (x * x * x))
    return 0.5 * x * (1.0 + jnp.tanh(inner))


def _peer_kernel(h_ref, u_ref, v_ref, rank_ref, d_ref, n_ref, c_ref, x1_ref, fg_ref,
                 o_ref, acc_ref, *, groups):
    e = pl.program_id(1)

    @pl.when(e == 0)
    def _():
        acc_ref[...] = jnp.zeros_like(acc_ref)

    act = _gelu_tanh(_dot_nt(u_ref[...], h_ref[...]))
    parts = []
    for la in range(groups):
        a = e * groups + la
        gate = jnp.zeros((N_KEYS, act.shape[1]), F32)
        for h in range(PEER_HEADS):
            n_row = n_ref[h, pl.ds(a, 1), :]
            c_row = c_ref[h, pl.ds(a, 1), :]
            gate = gate + jnp.where(rank_ref[h] < n_row, d_ref[h], 0.0) * c_row
        parts.append((gate * act[la * N_KEYS:(la + 1) * N_KEYS]).astype(BF16))
    w = jnp.concatenate(parts, axis=0)
    acc_ref[...] += _dot_tn(w, v_ref[...])

    @pl.when(e == pl.num_programs(1) - 1)
    def _():
        y = x1_ref[...] + acc_ref[...]
        o_ref[...] = y * _rms_scale(y) * fg_ref[...]


def _peer_ffn(h2, u, v, rank, dd, nn, cc, x1, fg, tm, te):
    t, d = h2.shape
    n_exp = u.shape[0]
    heads = rank.shape[0]
    tok = lambda i, e: (i, 0)
    tbl_spec = pl.BlockSpec((heads, N_KEYS, tm), lambda i, e: (0, 0, i))
    return pl.pallas_call(
        functools.partial(_peer_kernel, groups=te // N_KEYS),
        grid=(t // tm, n_exp // te),
        in_specs=[pl.BlockSpec((tm, d), tok),
                  pl.BlockSpec((te, d), lambda i, e: (e, 0)),
                  pl.BlockSpec((te, d), lambda i, e: (e, 0)),
                  tbl_spec, tbl_spec, tbl_spec, tbl_spec,
                  pl.BlockSpec((tm, d), tok),
                  pl.BlockSpec((1, d), lambda i, e: (0, 0))],
        out_specs=pl.BlockSpec((tm, d), tok),
        out_shape=jax.ShapeDtypeStruct((t, d), F32),
        scratch_shapes=[pltpu.VMEM((tm, d), F32)],
        compiler_params=pltpu.CompilerParams(dimension_semantics=("parallel", "arbitrary"),
                                             vmem_limit_bytes=VMEM_LIMIT),
        name="peer_ffn",
    )(h2, u, v, rank, dd, nn, cc, x1, fg)


def kernel(x, norm1_g, w_in, sb_out_g, swa_out_g, swa_sinks, w_out, norm2_g, peer_w_q,
           peer_keys1, peer_keys2, peer_u, peer_v, final_g):
    b, s, d = x.shape
    assert norm1_g.shape[0] == 1, "single-layer block"
    assert s % SB_KEY_WINDOW == 0
    t = b * s
    xt = x.reshape(t, d)

    proj = _in_proj(xt, norm1_g[0].reshape(1, d), w_in[0].astype(BF16), tm=512)
    proj3 = proj.reshape(b, s, proj.shape[1])
    sbm = _sb_attention(proj3, sb_out_g[0].reshape(1, D_SB))
    swm = _swa_attention(proj3, swa_sinks[0].reshape(1, SWA_HEADS), swa_out_g[0].reshape(1, D_SWA))

    w_o = w_out[0].astype(BF16)
    x1, h2 = _out_proj(xt, sbm.reshape(t, D_SB), swm.reshape(t, D_SWA), w_o[:D_SB], w_o[D_SB:],
                       norm2_g[0].reshape(1, d), tm=512)

    rank, dd, nn, cc = _peer_route(h2, peer_w_q[0].astype(BF16), peer_keys1[0].astype(BF16),
                                   peer_keys2[0].astype(BF16), tm=256)
    out = _peer_ffn(h2, peer_u[0].astype(BF16), peer_v[0].astype(BF16), rank, dd, nn, cc, x1,
                    final_g.reshape(1, d), tm=512, te=512)
    return out.reshape(b, s, d)
```
